```python
import math
import jax, jax.numpy as jnp
from jax import lax
import numpy as np

D_MODEL = 1024
BATCH = 4
SEQ = 8192
DEPTH = 4

CHUNK = 64
MIX_WIDTH = D_MODEL
CONV_WIDTH = MIX_WIDTH // 2
CONV_GROUPS = 8
CONV_K = 31
RET_HEADS = 4
RET_DIM = (MIX_WIDTH - CONV_WIDTH) // RET_HEADS
RET_WIDTH = RET_HEADS * RET_DIM
IN_WIDTH = 2 * CONV_WIDTH + 4 * RET_WIDTH
D_FF = int(math.ceil(8 * D_MODEL / 3 / 256) * 256)
ROPE_BASE = 10000.0
EPS = 1e-6

kernel_name = 'hybrid_conformer_retention_encoder'


def rms_norm(x, g):
    xf = x.astype(jnp.float32)
    y = xf * lax.rsqrt(jnp.mean(xf * xf, axis=-1, keepdims=True) + EPS)
    return y.astype(x.dtype) * g


def layer_norm(x, g, b):
    xf = x.astype(jnp.float32)
    mu = jnp.mean(xf, axis=-1, keepdims=True)
    var = jnp.mean(jnp.square(xf - mu), axis=-1, keepdims=True)
    return ((xf - mu) * lax.rsqrt(var + EPS)).astype(x.dtype) * g + b


def causal_depthwise_conv(u, w, b):
    C = u.shape[-1]
    up = jnp.pad(u, ((0, 0), (CONV_K - 1, 0), (0, 0)))
    y = lax.conv_general_dilated(up, w[:, None, :].astype(u.dtype), window_strides=(1,),
                                 padding='VALID', dimension_numbers=('NWC', 'WIO', 'NWC'),
                                 feature_group_count=C)
    return y + b


def rotary(t, pos):
    half = t.shape[-1] // 2
    freqs = ROPE_BASE ** (-jnp.arange(half, dtype=jnp.float32) / half)
    ang = pos[:, None] * freqs[None, :]
    cos = jnp.cos(ang)[None, :, None, :]
    sin = jnp.sin(ang)[None, :, None, :]
    t1, t2 = t[..., :half], t[..., half:]
    return jnp.concatenate([t1 * cos - t2 * sin, t1 * sin + t2 * cos], axis=-1)


def chunk_retention(q, k, v):
    Bsz, S, H, Dk = q.shape
    Dv = v.shape[-1]
    NC = S // CHUNK

    def blk(t):
        return t.reshape(Bsz, NC, CHUNK, H, t.shape[-1]).transpose(0, 3, 1, 2, 4)

    q, k, v = blk(q), blk(k), blk(v)
    log_g = jnp.log(1.0 - 2.0 ** (-5.0 - jnp.arange(H, dtype=jnp.float32)))
    idx = jnp.arange(CHUNK, dtype=jnp.float32)
    intra_decay = jnp.exp(log_g[:, None, None] * jnp.abs(idx[:, None] - idx[None, :]))
    scores = jnp.einsum('bhnid,bhnjd->bhnij', q, k) * intra_decay[None, :, None]
    intra = jnp.einsum('bhnij,bhnje->bhnie', scores, v)

    k_dec = k * jnp.exp(log_g[:, None] * (CHUNK - 1 - idx))[None, :, None, :, None]
    kv = jnp.einsum('bhnjd,bhnje->nbhde', k_dec, v)
    chunk_decay = jnp.exp(log_g * CHUNK)[None, :, None, None]

    def step(state, kv_n):
        return chunk_decay * state + kv_n, state

    _, prev = lax.scan(step, jnp.zeros((Bsz, H, Dk, Dv), jnp.float32), kv)
    q_dec = q * jnp.exp(log_g[:, None] * (idx + 1.0))[None, :, None, :, None]
    cross = jnp.einsum('bhnid,nbhde->bhnie', q_dec, prev)
    return (intra + cross).transpose(0, 2, 3, 1, 4).reshape(Bsz, S, H, Dv)


def setup_inputs(seed: int = 0) -> dict:
    key = jax.random.key(seed)
    ks = jax.random.split(key, 16)
    f32 = jnp.float32
    n = lambda k, shape, s: jax.random.normal(k, shape, f32) * s
    return {
        'x': n(ks[0], (BATCH, SEQ, D_MODEL), 1.0),
        'norm1_g': 1.0 + n(ks[1], (DEPTH, D_MODEL), 0.02),
        'w_in': n(ks[2], (DEPTH, D_MODEL, IN_WIDTH), D_MODEL ** -0.5),
        'conv_w': n(ks[3], (DEPTH, CONV_K, CONV_WIDTH), CONV_K ** -0.5),
        'conv_b': n(ks[4], (DEPTH, CONV_WIDTH), 0.01),
        'conv_ln_g': 1.0 + n(ks[5], (DEPTH, CONV_WIDTH), 0.02),
        'conv_ln_b': n(ks[6], (DEPTH, CONV_WIDTH), 0.01),
        'ret_gn_g': 1.0 + n(ks[7], (DEPTH, RET_WIDTH), 0.02),
        'w_out': n(ks[8], (DEPTH, MIX_WIDTH, D_MODEL), (MIX_WIDTH * 2 * DEPTH) ** -0.5),
        'norm2_g': 1.0 + n(ks[9], (DEPTH, D_MODEL), 0.02),
        'w_gate': n(ks[10], (DEPTH, D_MODEL, D_FF), D_MODEL ** -0.5),
        'w_up': n(ks[11], (DEPTH, D_MODEL, D_FF), D_MODEL ** -0.5),
        'w_down': n(ks[12], (DEPTH, D_FF, D_MODEL), (D_FF * 2 * DEPTH) ** -0.5),
        'final_g': 1.0 + n(ks[13], (D_MODEL,), 0.02),
    }


def reference(x, norm1_g, w_in, conv_w, conv_b, conv_ln_g, conv_ln_b, ret_gn_g, w_out,
              norm2_g, w_gate, w_up, w_down, final_g):
    Bsz, S, _ = x.shape
    pos = jnp.arange(S, dtype=jnp.float32)
    cw, rw = CONV_WIDTH, RET_WIDTH
    for l in range(DEPTH):
        h = rms_norm(x, norm1_g[l])
        proj = h @ w_in[l]
        a = proj[..., :cw]
        b = proj[..., cw:2 * cw]
        q = proj[..., 2 * cw:2 * cw + rw]
        k = proj[..., 2 * cw + rw:2 * cw + 2 * rw]
        v = proj[..., 2 * cw + 2 * rw:2 * cw + 3 * rw]
        g = proj[..., 2 * cw + 3 * rw:]

        u = a * jax.nn.sigmoid(b)
        u = causal_depthwise_conv(u, conv_w[l], conv_b[l])
        u = jax.nn.silu(layer_norm(u, conv_ln_g[l], conv_ln_b[l]))

        qh = rotary(q.reshape(Bsz, S, RET_HEADS, RET_DIM).astype(jnp.float32), pos)
        kh = rotary(k.reshape(Bsz, S, RET_HEADS, RET_DIM).astype(jnp.float32), pos) * (RET_DIM ** -0.5)
        vh = v.reshape(Bsz, S, RET_HEADS, RET_DIM).astype(jnp.float32)
        r = chunk_retention(qh, kh, vh)
        mu = jnp.mean(r, axis=-1, keepdims=True)
        var = jnp.mean(jnp.square(r - mu), axis=-1, keepdims=True)
        r = ((r - mu) * lax.rsqrt(var + EPS)).reshape(Bsz, S, rw).astype(x.dtype)
        r = r * ret_gn_g[l] * jax.nn.silu(g)

        mixed = jnp.concatenate([u, r], axis=-1)
        x = x + mixed @ w_out[l]

        h2 = rms_norm(x, norm2_g[l])
        x = x + (jax.nn.silu(h2 @ w_gate[l]) * (h2 @ w_up[l])) @ w_down[l]
    return rms_norm(x, final_g)
```

```python
import functools
import math

import jax
import jax.numpy as jnp
from jax import lax
from jax.experimental import pallas as pl
from jax.experimental.pallas import tpu as pltpu

D_MODEL = 1024
CHUNK = 64
CONV_WIDTH = 512
CONV_K = 31
RET_HEADS = 4
RET_DIM = 128
RET_WIDTH = RET_HEADS * RET_DIM
IN_WIDTH = 2 * CONV_WIDTH + 4 * RET_WIDTH
D_FF = 2816
ROPE_BASE = 10000.0
EPS = 1e-6

V7X_LANES = 128
V7X_SUBLANES = 8
V7X_MXU_COLS = 256
V7X_VMEM_BYTES = 64 * 1024 * 1024

SEQ_TILE = 512
RET_TILE = 256
FFN_TILE = 512
HIST = 32
CONV_ROWS = 64
NCOL = V7X_MXU_COLS


def _dot(a, b):
    return jnp.dot(a, b, preferred_element_type=jnp.float32)


def _sigmoid(x):
    return 1.0 / (1.0 + jnp.exp(-x))


def _silu(x):
    return x * _sigmoid(x)


def _rms_rows(x, g):
    ms = jnp.mean(x * x, axis=-1, keepdims=True)
    return x * lax.rsqrt(ms + EPS) * g


def _mixer_kernel(x_ref, g1_ref, w_in_ref, conv_w_ref, conv_b_ref, ln_g_ref, ln_b_ref,
                  gn_g_ref, w_out_ref, cos_ref, sin_ref, mask_ref, qdec_ref, kdec_ref,
                  sdec_ref, o_ref, h_scr, proj_scr, u_scr, state_scr, mixed_scr):
    t = pl.program_id(1)
    ts = x_ref.shape[0]
    cw, rw = CONV_WIDTH, RET_WIDTH

    @pl.when(t == 0)
    def _():
        u_scr[0:HIST, :] = jnp.zeros((HIST, cw), jnp.float32)
        state_scr[...] = jnp.zeros_like(state_scr)

    @pl.when(t != 0)
    def _():
        u_scr[0:HIST, :] = u_scr[ts:ts + HIST, :]

    h_scr[...] = _rms_rows(x_ref[...], g1_ref[...]).astype(jnp.bfloat16)
    for c in range(IN_WIDTH // NCOL):
        cs = slice(c * NCOL, (c + 1) * NCOL)
        proj_scr[:, cs] = _dot(h_scr[...], w_in_ref[:, cs])

    u_scr[HIST:HIST + ts, :] = proj_scr[:, 0:cw] * _sigmoid(proj_scr[:, cw:2 * cw])
    first = HIST - (CONV_K - 1)
    for r in range(ts // CONV_ROWS):
        r0 = r * CONV_ROWS
        parts = []
        for lb in range(cw // V7X_LANES):
            ls = slice(lb * V7X_LANES, (lb + 1) * V7X_LANES)
            acc = jnp.broadcast_to(conv_b_ref[:, ls], (CONV_ROWS, V7X_LANES))
            for k in range(CONV_K):
                acc = acc + conv_w_ref[k:k + 1, ls] * u_scr[r0 + first + k:r0 + first + k + CONV_ROWS, ls]
            parts.append(acc)
        y = jnp.concatenate(parts, axis=-1)
        mu = jnp.mean(y, axis=-1, keepdims=True)
        yc = y - mu
        var = jnp.mean(yc * yc, axis=-1, keepdims=True)
        yn = yc * lax.rsqrt(var + EPS) * ln_g_ref[...] + ln_b_ref[...]
        mixed_scr[r0:r0 + CONV_ROWS, 0:cw] = _silu(yn).astype(jnp.bfloat16)

    qo, ko, vo, go = 2 * cw, 2 * cw + rw, 2 * cw + 2 * rw, 2 * cw + 3 * rw
    for s in range(ts // RET_TILE):
        rs = slice(s * RET_TILE, (s + 1) * RET_TILE)
        cos = cos_ref[rs, :]
        sin = sin_ref[rs, :]
        for hd in range(RET_HEADS):
            hs = slice(hd * RET_DIM, (hd + 1) * RET_DIM)
            q = proj_scr[rs, qo + hd * RET_DIM:qo + (hd + 1) * RET_DIM]
            k = proj_scr[rs, ko + hd * RET_DIM:ko + (hd + 1) * RET_DIM]
            v = proj_scr[rs, vo + hd * RET_DIM:vo + (hd + 1) * RET_DIM].astype(jnp.bfloat16)
            g = proj_scr[rs, go + hd * RET_DIM:go + (hd + 1) * RET_DIM]
            q = q * cos + pltpu.roll(q, RET_DIM // 2, axis=1) * sin
            k = (k * cos + pltpu.roll(k, RET_DIM // 2, axis=1) * sin) * (RET_DIM ** -0.5)
            scores = lax.dot_general(q.astype(jnp.bfloat16), k.astype(jnp.bfloat16),
                                     (((1,), (1,)), ((), ())),
                                     preferred_element_type=jnp.float32)
            p = (scores * mask_ref[hd]).astype(jnp.bfloat16)
            state = state_scr[:, hs]
            r = _dot(p, v) + _dot((q * qdec_ref[:, hs]).astype(jnp.bfloat16),
                                  state.astype(jnp.bfloat16))
            kv = lax.dot_general((k * kdec_ref[:, hs]).astype(jnp.bfloat16), v,
                                 (((0,), (0,)), ((), ())),
                                 preferred_element_type=jnp.float32)
            state_scr[:, hs] = sdec_ref[:, hs] * state + kv
            mu = jnp.mean(r, axis=-1, keepdims=True)
            rc = r - mu
            var = jnp.mean(rc * rc, axis=-1, keepdims=True)
            rn = rc * lax.rsqrt(var + EPS)
            mixed_scr[rs, cw + hd * RET_DIM:cw + (hd + 1) * RET_DIM] = (
                rn * gn_g_ref[:, hs] * _silu(g)).astype(jnp.bfloat16)

    for c in range(D_MODEL // NCOL):
        cs = slice(c * NCOL, (c + 1) * NCOL)
        o_ref[:, cs] = x_ref[:, cs] + _dot(mixed_scr[...], w_out_ref[:, cs])


def _ffn_kernel(x_ref, g2_ref, w_gate_ref, w_up_ref, w_down_ref, gf_ref, o_ref,
                h_scr, a_scr, *, final_norm):
    h_scr[...] = _rms_rows(x_ref[...], g2_ref[...]).astype(jnp.bfloat16)
    for c in range(D_FF // NCOL):
        cs = slice(c * NCOL, (c + 1) * NCOL)
        gate = _dot(h_scr[...], w_gate_ref[:, cs])
        up = _dot(h_scr[...], w_up_ref[:, cs])
        a_scr[:, cs] = (_silu(gate) * up).astype(jnp.bfloat16)
    for c in range(D_MODEL // NCOL):
        cs = slice(c * NCOL, (c + 1) * NCOL)
        o_ref[:, cs] = x_ref[:, cs] + _dot(a_scr[...], w_down_ref[:, cs])
    if final_norm:
        o_ref[...] = _rms_rows(o_ref[...], gf_ref[...])


def _const_spec(shape):
    nd = len(shape)
    return pl.BlockSpec(shape, lambda *_: (0,) * nd, pipeline_mode=pl.Buffered(1))


def _vmem_limit(block_bytes):
    return min(int(block_bytes * 1.25) + (8 << 20), V7X_VMEM_BYTES - (6 << 20))


def _mixer_call(x, g1, w_in, conv_w, conv_b, ln_g, ln_b, gn_g, w_out, cos2, sin2,
                mask, qdec, kdec, sdec):
    bsz, seq, d = x.shape
    ts = SEQ_TILE
    assert seq % ts == 0 and ts % RET_TILE == 0 and ts % CONV_ROWS == 0
    tile = pl.BlockSpec((None, ts, d), lambda b, t: (b, t, 0))
    tab = pl.BlockSpec((ts, RET_DIM), lambda b, t: (t, 0))
    consts = [g1, w_in, conv_w, conv_b, ln_g, ln_b, gn_g, w_out]
    tables = [mask, qdec, kdec, sdec]
    scratch = [
        pltpu.VMEM((ts, d), jnp.bfloat16),
        pltpu.VMEM((ts, IN_WIDTH), jnp.float32),
        pltpu.VMEM((HIST + ts, CONV_WIDTH), jnp.float32),
        pltpu.VMEM((RET_DIM, RET_WIDTH), jnp.float32),
        pltpu.VMEM((ts, d), jnp.bfloat16),
    ]
    nbytes = (4 * ts * d * 4 + sum(a.size * a.dtype.itemsize for a in consts + tables)
              + 4 * ts * RET_DIM * 4 + 2 * ts * d * 2 + ts * IN_WIDTH * 4
              + (HIST + ts) * CONV_WIDTH * 4 + RET_DIM * RET_WIDTH * 4)
    return pl.pallas_call(
        _mixer_kernel,
        grid=(bsz, seq // ts),
        in_specs=([tile] + [_const_spec(a.shape) for a in consts] + [tab, tab]
                  + [_const_spec(a.shape) for a in tables]),
        out_specs=tile,
        out_shape=jax.ShapeDtypeStruct(x.shape, x.dtype),
        scratch_shapes=scratch,
        compiler_params=pltpu.CompilerParams(
            dimension_semantics=("arbitrary", "arbitrary"),
            vmem_limit_bytes=_vmem_limit(nbytes)),
        name="mixer",
    )(x, *consts, cos2, sin2, *tables)


def _ffn_call(x2d, g2, w_gate, w_up, w_down, gf, final_norm):
    n, d = x2d.shape
    tm = FFN_TILE
    assert n % tm == 0
    tile = pl.BlockSpec((tm, d), lambda i: (i, 0))
    consts = [g2, w_gate, w_up, w_down, gf]
    nbytes = (4 * tm * d * 4 + sum(a.size * a.dtype.itemsize for a in consts)
              + tm * d * 2 + tm * D_FF * 2)
    return pl.pallas_call(
        functools.partial(_ffn_kernel, final_norm=final_norm),
        grid=(n // tm,),
        in_specs=[tile] + [_const_spec(a.shape) for a in consts],
        out_specs=tile,
        out_shape=jax.ShapeDtypeStruct(x2d.shape, x2d.dtype),
        scratch_shapes=[pltpu.VMEM((tm, d), jnp.bfloat16),
                        pltpu.VMEM((tm, D_FF), jnp.bfloat16)],
        compiler_params=pltpu.CompilerParams(
            dimension_semantics=("arbitrary",),
            vmem_limit_bytes=_vmem_limit(nbytes)),
        name="ffn",
    )(x2d, *consts)


def _tables(seq):
    f32 = jnp.float32
    half = RET_DIM // 2
    pos = jnp.arange(seq, dtype=f32)
    freqs = ROPE_BASE ** (-jnp.arange(half, dtype=f32) / half)
    ang = pos[:, None] * freqs[None, :]
    cos, sin = jnp.cos(ang), jnp.sin(ang)
    cos2 = jnp.concatenate([cos, cos], axis=-1)
    sin2 = jnp.concatenate([-sin, sin], axis=-1)

    log_g = jnp.log(1.0 - 2.0 ** (-5.0 - jnp.arange(RET_HEADS, dtype=f32)))
    idx = jnp.arange(RET_TILE, dtype=f32)
    diff = idx[:, None] - idx[None, :]
    chunk = jnp.arange(RET_TILE) // CHUNK
    same = chunk[:, None] == chunk[None, :]
    earlier = chunk[:, None] > chunk[None, :]
    lg = log_g[:, None, None]
    mask = jnp.where(same[None], jnp.exp(lg * jnp.abs(diff)[None]),
                     jnp.where(earlier[None], jnp.exp(lg * diff[None]), 0.0))

    def lanes(per_head):
        return jnp.repeat(per_head.T, RET_DIM, axis=1)

    qdec = lanes(jnp.exp(log_g[:, None] * (idx[None, :] + 1.0)))
    kdec = lanes(jnp.exp(log_g[:, None] * (RET_TILE - 1.0 - idx[None, :])))
    sdec = jnp.repeat(jnp.exp(log_g * RET_TILE), RET_DIM)[None, :]
    return cos2, sin2, mask, qdec, kdec, sdec


def kernel(x, norm1_g, w_in, conv_w, conv_b, conv_ln_g, conv_ln_b, ret_gn_g, w_out,
           norm2_g, w_gate, w_up, w_down, final_g):
    bsz, seq, d = x.shape
    depth = w_in.shape[0]
    bf16 = jnp.bfloat16
    tables = _tables(seq)
    row = lambda a: a.reshape(1, -1)
    for l in range(depth):
        x = _mixer_call(x, row(norm1_g[l]), w_in[l].astype(bf16), conv_w[l], row(conv_b[l]),
                        row(conv_ln_g[l]), row(conv_ln_b[l]), row(ret_gn_g[l]),
                        w_out[l].astype(bf16), *tables)
        x = _ffn_call(x.reshape(bsz * seq, d), row(norm2_g[l]), w_gate[l].astype(bf16),
                      w_up[l].astype(bf16), w_down[l].astype(bf16), row(final_g),
                      final_norm=(l == depth - 1)).reshape(bsz, seq, d)
    return x
```

```python
import functools
import math

import jax
import jax.numpy as jnp
from jax import lax
from jax.experimental import pallas as pl
from jax.experimental.pallas import tpu as pltpu

D_MODEL = 1024
CHUNK = 64
CONV_WIDTH = 512
CONV_K = 31
RET_HEADS = 4
RET_DIM = 128
RET_WIDTH = RET_HEADS * RET_DIM
IN_WIDTH = 2 * CONV_WIDTH + 4 * RET_WIDTH
D_FF = 2816
ROPE_BASE = 10000.0
EPS = 1e-6

V7X_LANES = 128
V7X_SUBLANES = 8
V7X_MXU_COLS = 256
V7X_VMEM_BYTES = 64 * 1024 * 1024

SEQ_TILE = 512
RET_TILE = 256
FFN_TILE = 512
HIST = 32
CONV_ROWS = 64
NCOL = V7X_MXU_COLS


def _dot(a, b):
    return jnp.dot(a, b, preferred_element_type=jnp.float32)


def _sigmoid(x):
    return 1.0 / (1.0 + jnp.exp(-x))


def _silu(x):
    return x * _sigmoid(x)


def _rms_rows(x, g):
    ms = jnp.mean(x * x, axis=-1, keepdims=True)
    return x * lax.rsqrt(ms + EPS) * g


def _mixer_kernel(x_ref, g1_ref, w_in_ref, conv_w_ref, conv_b_ref, ln_g_ref, ln_b_ref,
                  gn_g_ref, w_out_ref, cos_ref, sin_ref, mask_ref, qdec_ref, kdec_ref,
                  sdec_ref, o_ref, h_scr, proj_scr, u_scr, state_scr, mixed_scr):
    t = pl.program_id(1)
    ts = x_ref.shape[0]
    cw, rw = CONV_WIDTH, RET_WIDTH

    @pl.when(t == 0)
    def _():
        u_scr[:, 0:HIST, :] = jnp.zeros((cw // V7X_LANES, HIST, V7X_LANES), jnp.float32)
        state_scr[...] = jnp.zeros_like(state_scr)

    @pl.when(t != 0)
    def _():
        u_scr[:, 0:HIST, :] = u_scr[:, ts:ts + HIST, :]

    h_scr[...] = _rms_rows(x_ref[...], g1_ref[...]).astype(jnp.bfloat16)
    for c in range(IN_WIDTH // NCOL):
        cs = slice(c * NCOL, (c + 1) * NCOL)
        proj_scr[:, cs] = _dot(h_scr[...], w_in_ref[:, cs])

    for lb in range(cw // V7X_LANES):
        a = proj_scr[:, lb * V7X_LANES:(lb + 1) * V7X_LANES]
        b = proj_scr[:, cw + lb * V7X_LANES:cw + (lb + 1) * V7X_LANES]
        u_scr[lb, HIST:HIST + ts, :] = a * _sigmoid(b)
    first = HIST - (CONV_K - 1)
    for r in range(ts // CONV_ROWS):
        r0 = r * CONV_ROWS
        parts = []
        for lb in range(cw // V7X_LANES):
            ls = slice(lb * V7X_LANES, (lb + 1) * V7X_LANES)
            acc = jnp.broadcast_to(conv_b_ref[:, ls], (CONV_ROWS, V7X_LANES))
            for k in range(CONV_K):
                acc = acc + conv_w_ref[k:k + 1, ls] * u_scr[lb, r0 + first + k:r0 + first + k + CONV_ROWS, :]
            parts.append(acc)
        y = jnp.concatenate(parts, axis=-1)
        mu = jnp.mean(y, axis=-1, keepdims=True)
        yc = y - mu
        var = jnp.mean(yc * yc, axis=-1, keepdims=True)
        yn = yc * lax.rsqrt(var + EPS) * ln_g_ref[...] + ln_b_ref[...]
        mixed_scr[r0:r0 + CONV_ROWS, 0:cw] = _silu(yn).astype(jnp.bfloat16)

    qo, ko, vo, go = 2 * cw, 2 * cw + rw, 2 * cw + 2 * rw, 2 * cw + 3 * rw
    for s in range(ts // RET_TILE):
        rs = slice(s * RET_TILE, (s + 1) * RET_TILE)
        cos = cos_ref[rs, :]
        sin = sin_ref[rs, :]
        for hd in range(RET_HEADS):
            hs = slice(hd * RET_DIM, (hd + 1) * RET_DIM)
            q = proj_scr[rs, qo + hd * RET_DIM:qo + (hd + 1) * RET_DIM]
            k = proj_scr[rs, ko + hd * RET_DIM:ko + (hd + 1) * RET_DIM]
            v = proj_scr[rs, vo + hd * RET_DIM:vo + (hd + 1) * RET_DIM].astype(jnp.bfloat16)
            g = proj_scr[rs, go + hd * RET_DIM:go + (hd + 1) * RET_DIM]
            q = q * cos + pltpu.roll(q, RET_DIM // 2, axis=1) * sin
            k = (k * cos + pltpu.roll(k, RET_DIM // 2, axis=1) * sin) * (RET_DIM ** -0.5)
            scores = lax.dot_general(q.astype(jnp.bfloat16), k.astype(jnp.bfloat16),
                                     (((1,), (1,)), ((), ())),
                                     preferred_element_type=jnp.float32)
            p = (scores * mask_ref[hd]).astype(jnp.bfloat16)
            state = state_scr[:, hs]
            r = _dot(p, v) + _dot((q * qdec_ref[:, hs]).astype(jnp.bfloat16),
                                  state.astype(jnp.bfloat16))
            kv = lax.dot_general((k * kdec_ref[:, hs]).astype(jnp.bfloat16), v,
                                 (((0,), (0,)), ((), ())),
                                 preferred_element_type=jnp.float32)
            state_scr[:, hs] = sdec_ref[:, hs] * state + kv
            mu = jnp.mean(r, axis=-1, keepdims=True)
            rc = r - mu
            var = jnp.mean(rc * rc, axis=-1, keepdims=True)
            rn = rc * lax.rsqrt(var + EPS)
            mixed_scr[rs, cw + hd * RET_DIM:cw + (hd + 1) * RET_DIM] = (
                rn * gn_g_ref[:, hs] * _silu(g)).astype(jnp.bfloat16)

    for c in range(D_MODEL // NCOL):
        cs = slice(c * NCOL, (c + 1) * NCOL)
        o_ref[:, cs] = x_ref[:, cs] + _dot(mixed_scr[...], w_out_ref[:, cs])


def _ffn_kernel(x_ref, g2_ref, w_gate_ref, w_up_ref, w_down_ref, gf_ref, o_ref,
                h_scr, a_scr, *, final_norm):
    h_scr[...] = _rms_rows(x_ref[...], g2_ref[...]).astype(jnp.bfloat16)
    for c in range(D_FF // NCOL):
        cs = slice(c * NCOL, (c + 1) * NCOL)
        gate = _dot(h_scr[...], w_gate_ref[:, cs])
        up = _dot(h_scr[...], w_up_ref[:, cs])
        a_scr[:, cs] = (_silu(gate) * up).astype(jnp.bfloat16)
    for c in range(D_MODEL // NCOL):
        cs = slice(c * NCOL, (c + 1) * NCOL)
        o_ref[:, cs] = x_ref[:, cs] + _dot(a_scr[...], w_down_ref[:, cs])
    if final_norm:
        o_ref[...] = _rms_rows(o_ref[...], gf_ref[...])


def _const_spec(shape):
    nd = len(shape)
    return pl.BlockSpec(shape, lambda *_: (0,) * nd, pipeline_mode=pl.Buffered(1))


def _vmem_limit(block_bytes):
    return min(int(block_bytes * 1.25) + (8 << 20), V7X_VMEM_BYTES - (6 << 20))


def _mixer_call(x, g1, w_in, conv_w, conv_b, ln_g, ln_b, gn_g, w_out, cos2, sin2,
                mask, qdec, kdec, sdec):
    bsz, seq, d = x.shape
    ts = SEQ_TILE
    assert seq % ts == 0 and ts % RET_TILE == 0 and ts % CONV_ROWS == 0
    tile = pl.BlockSpec((None, ts, d), lambda b, t: (b, t, 0))
    tab = pl.BlockSpec((ts, RET_DIM), lambda b, t: (t, 0))
    consts = [g1, w_in, conv_w, conv_b, ln_g, ln_b, gn_g, w_out]
    tables = [mask, qdec, kdec, sdec]
    scratch = [
        pltpu.VMEM((ts, d), jnp.bfloat16),
        pltpu.VMEM((ts, IN_WIDTH), jnp.float32),
        pltpu.VMEM((CONV_WIDTH // V7X_LANES, HIST + ts, V7X_LANES), jnp.float32),
        pltpu.VMEM((RET_DIM, RET_WIDTH), jnp.float32),
        pltpu.VMEM((ts, d), jnp.bfloat16),
    ]
    nbytes = (4 * ts * d * 4 + sum(a.size * a.dtype.itemsize for a in consts + tables)
              + 4 * ts * RET_DIM * 4 + 2 * ts * d * 2 + ts * IN_WIDTH * 4
              + (HIST + ts) * CONV_WIDTH * 4 + RET_DIM * RET_WIDTH * 4)
    return pl.pallas_call(
        _mixer_kernel,
        grid=(bsz, seq // ts),
        in_specs=([tile] + [_const_spec(a.shape) for a in consts] + [tab, tab]
                  + [_const_spec(a.shape) for a in tables]),
        out_specs=tile,
        out_shape=jax.ShapeDtypeStruct(x.shape, x.dtype),
        scratch_shapes=scratch,
        compiler_params=pltpu.CompilerParams(
            dimension_semantics=("arbitrary", "arbitrary"),
            vmem_limit_bytes=_vmem_limit(nbytes)),
        name="mixer",
    )(x, *consts, cos2, sin2, *tables)


def _ffn_call(x2d, g2, w_gate, w_up, w_down, gf, final_norm):
    n, d = x2d.shape
    tm = FFN_TILE
    assert n % tm == 0
    tile = pl.BlockSpec((tm, d), lambda i: (i, 0))
    consts = [g2, w_gate, w_up, w_down, gf]
    nbytes = (4 * tm * d * 4 + sum(a.size * a.dtype.itemsize for a in consts)
              + tm * d * 2 + tm * D_FF * 2)
    return pl.pallas_call(
        functools.partial(_ffn_kernel, final_norm=final_norm),
        grid=(n // tm,),
        in_specs=[tile] + [_const_spec(a.shape) for a in consts],
        out_specs=tile,
        out_shape=jax.ShapeDtypeStruct(x2d.shape, x2d.dtype),
        scratch_shapes=[pltpu.VMEM((tm, d), jnp.bfloat16),
                        pltpu.VMEM((tm, D_FF), jnp.bfloat16)],
        compiler_params=pltpu.CompilerParams(
            dimension_semantics=("arbitrary",),
            vmem_limit_bytes=_vmem_limit(nbytes)),
        name="ffn",
    )(x2d, *consts)


def _tables(seq):
    f32 = jnp.float32
    half = RET_DIM // 2
    pos = jnp.arange(seq, dtype=f32)
    freqs = ROPE_BASE ** (-jnp.arange(half, dtype=f32) / half)
    ang = pos[:, None] * freqs[None, :]
    cos, sin = jnp.cos(ang), jnp.sin(ang)
    cos2 = jnp.concatenate([cos, cos], axis=-1)
    sin2 = jnp.concatenate([-sin, sin], axis=-1)

    log_g = jnp.log(1.0 - 2.0 ** (-5.0 - jnp.arange(RET_HEADS, dtype=f32)))
    idx = jnp.arange(RET_TILE, dtype=f32)
    diff = idx[:, None] - idx[None, :]
    chunk = jnp.arange(RET_TILE) // CHUNK
    same = chunk[:, None] == chunk[None, :]
    earlier = chunk[:, None] > chunk[None, :]
    lg = log_g[:, None, None]
    mask = jnp.where(same[None], jnp.exp(lg * jnp.abs(diff)[None]),
                     jnp.where(earlier[None], jnp.exp(lg * diff[None]), 0.0))

    def lanes(per_head):
        return jnp.repeat(per_head.T, RET_DIM, axis=1)

    qdec = lanes(jnp.exp(log_g[:, None] * (idx[None, :] + 1.0)))
    kdec = lanes(jnp.exp(log_g[:, None] * (RET_TILE - 1.0 - idx[None, :])))
    sdec = jnp.repeat(jnp.exp(log_g * RET_TILE), RET_DIM)[None, :]
    return cos2, sin2, mask, qdec, kdec, sdec


def kernel(x, norm1_g, w_in, conv_w, conv_b, conv_ln_g, conv_ln_b, ret_gn_g, w_out,
           norm2_g, w_gate, w_up, w_down, final_g):
    bsz, seq, d = x.shape
    depth = w_in.shape[0]
    bf16 = jnp.bfloat16
    tables = _tables(seq)
    row = lambda a: a.reshape(1, -1)
    for l in range(depth):
        x = _mixer_call(x, row(norm1_g[l]), w_in[l].astype(bf16), conv_w[l], row(conv_b[l]),
                        row(conv_ln_g[l]), row(conv_ln_b[l]), row(ret_gn_g[l]),
                        w_out[l].astype(bf16), *tables)
        x = _ffn_call(x.reshape(bsz * seq, d), row(norm2_g[l]), w_gate[l].astype(bf16),
                      w_up[l].astype(bf16), w_down[l].astype(bf16), row(final_g),
                      final_norm=(l == depth - 1)).reshape(bsz, seq, d)
    return x
```

```python
import functools

import jax
import jax.numpy as jnp
from jax import lax
from jax.experimental import pallas as pl
from jax.experimental.pallas import tpu as pltpu

D_MODEL = 1024
CHUNK = 64
CONV_WIDTH = 512
CONV_K = 31
RET_HEADS = 4
RET_DIM = 128
RET_WIDTH = RET_HEADS * RET_DIM
IN_WIDTH = 2 * CONV_WIDTH + 4 * RET_WIDTH
D_FF = 2816
ROPE_BASE = 10000.0
EPS = 1e-6

V7X_LANES = 128
V7X_MXU_COLS = 256
V7X_VMEM_BYTES = 64 * 1024 * 1024

SEQ_TILE = 512
RET_TILE = 256
FFN_TILE = 512
HIST = 32
CONV_ROWS = 64
NCOL = V7X_MXU_COLS
CONV_BLOCKS = CONV_WIDTH // V7X_LANES


def _dot(a, b):
    return jnp.dot(a, b, preferred_element_type=jnp.float32)


def _sigmoid(x):
    return 1.0 / (1.0 + jnp.exp(-x))


def _silu(x):
    return x * _sigmoid(x)


def _rms_rows(x, g):
    ms = jnp.mean(x * x, axis=-1, keepdims=True)
    return x * lax.rsqrt(ms + EPS) * g


def _mixer_kernel(x_ref, g1_ref, w_in_ref, conv_w_ref, conv_b_ref, ln_g_ref, ln_b_ref,
                  gn_g_ref, w_out_ref, cos_ref, sin_ref, mask_ref, qdec_ref, kdec_ref,
                  sdec_ref, o_ref, h_scr, u_scr, y_scr, state_scr, mixed_scr):
    t = pl.program_id(1)
    ts = x_ref.shape[0]
    cw = CONV_WIDTH

    @pl.when(t == 0)
    def _():
        u_scr[:, 0:HIST, :] = jnp.zeros((CONV_BLOCKS, HIST, V7X_LANES), jnp.float32)
        state_scr[...] = jnp.zeros_like(state_scr)

    @pl.when(t != 0)
    def _():
        u_scr[:, 0:HIST, :] = u_scr[:, ts:ts + HIST, :]

    h_scr[...] = _rms_rows(x_ref[...], g1_ref[...]).astype(jnp.bfloat16)

    def project(c):
        return _dot(h_scr[...], w_in_ref[:, c * NCOL:(c + 1) * NCOL])

    first = HIST - (CONV_K - 1)
    for lb in range(CONV_BLOCKS):
        ab = project(lb)
        u_scr[lb, HIST:HIST + ts, :] = ab[:, 0:V7X_LANES] * _sigmoid(ab[:, V7X_LANES:NCOL])
        ls = slice(lb * V7X_LANES, (lb + 1) * V7X_LANES)
        for r0 in range(0, ts, CONV_ROWS):
            acc = jnp.broadcast_to(conv_b_ref[:, ls], (CONV_ROWS, V7X_LANES))
            for k in range(CONV_K):
                acc = acc + conv_w_ref[k:k + 1, ls] * u_scr[lb, r0 + first + k:r0 + first + k + CONV_ROWS, :]
            y_scr[r0:r0 + CONV_ROWS, ls] = acc

    for hd in range(RET_HEADS):
        hs = slice(hd * RET_DIM, (hd + 1) * RET_DIM)
        qk = project(CONV_BLOCKS + 2 * hd)
        vg = project(CONV_BLOCKS + 2 * hd + 1)
        for blk in range(ts // RET_TILE):
            rs = slice(blk * RET_TILE, (blk + 1) * RET_TILE)
            cos = cos_ref[rs, :]
            sin = sin_ref[rs, :]
            q = qk[rs, 0:RET_DIM]
            k = qk[rs, RET_DIM:NCOL]
            v = vg[rs, 0:RET_DIM].astype(jnp.bfloat16)
            g = vg[rs, RET_DIM:NCOL]
            q = q * cos + pltpu.roll(q, RET_DIM // 2, axis=1) * sin
            k = k * cos + pltpu.roll(k, RET_DIM // 2, axis=1) * sin
            scores = lax.dot_general(q.astype(jnp.bfloat16), k.astype(jnp.bfloat16),
                                     (((1,), (1,)), ((), ())),
                                     preferred_element_type=jnp.float32)
            p = (scores * mask_ref[hd]).astype(jnp.bfloat16)
            state = state_scr[:, hs]
            r = _dot(p, v) + _dot((q * qdec_ref[:, hs]).astype(jnp.bfloat16),
                                  state.astype(jnp.bfloat16))
            kv = lax.dot_general((k * kdec_ref[:, hs]).astype(jnp.bfloat16), v,
                                 (((0,), (0,)), ((), ())),
                                 preferred_element_type=jnp.float32)
            state_scr[:, hs] = sdec_ref[:, hs] * state + kv
            mu = jnp.mean(r, axis=-1, keepdims=True)
            rc = r - mu
            var = jnp.mean(rc * rc, axis=-1, keepdims=True)
            rn = rc * lax.rsqrt(var + EPS)
            mixed_scr[rs, cw + hd * RET_DIM:cw + (hd + 1) * RET_DIM] = (
                rn * gn_g_ref[:, hs] * _silu(g)).astype(jnp.bfloat16)

    for r0 in range(0, ts, CONV_ROWS):
        y = y_scr[r0:r0 + CONV_ROWS, :]
        mu = jnp.mean(y, axis=-1, keepdims=True)
        yc = y - mu
        var = jnp.mean(yc * yc, axis=-1, keepdims=True)
        yn = yc * lax.rsqrt(var + EPS) * ln_g_ref[...] + ln_b_ref[...]
        mixed_scr[r0:r0 + CONV_ROWS, 0:cw] = _silu(yn).astype(jnp.bfloat16)

    for c in range(D_MODEL // NCOL):
        cs = slice(c * NCOL, (c + 1) * NCOL)
        o_ref[:, cs] = x_ref[:, cs] + _dot(mixed_scr[...], w_out_ref[:, cs])


def _ffn_kernel(x_ref, g2_ref, w_gate_ref, w_up_ref, w_down_ref, gf_ref, o_ref,
                h_scr, a_scr, *, final_norm):
    h_scr[...] = _rms_rows(x_ref[...], g2_ref[...]).astype(jnp.bfloat16)
    for c in range(D_FF // NCOL):
        cs = slice(c * NCOL, (c + 1) * NCOL)
        gate = _dot(h_scr[...], w_gate_ref[:, cs])
        up = _dot(h_scr[...], w_up_ref[:, cs])
        a_scr[:, cs] = (_silu(gate) * up).astype(jnp.bfloat16)
    for c in range(D_MODEL // NCOL):
        cs = slice(c * NCOL, (c + 1) * NCOL)
        o_ref[:, cs] = x_ref[:, cs] + _dot(a_scr[...], w_down_ref[:, cs])
    if final_norm:
        o_ref[...] = _rms_rows(o_ref[...], gf_ref[...])


def _const_spec(shape):
    nd = len(shape)
    return pl.BlockSpec(shape, lambda *_: (0,) * nd, pipeline_mode=pl.Buffered(1))


def _layer_spec(stacked, layer):
    nd = stacked.ndim - 1
    return pl.BlockSpec((None,) + stacked.shape[1:], lambda *_: (layer,) + (0,) * nd,
                        pipeline_mode=pl.Buffered(1))


def _layer_bytes(params):
    return sum(p.size // p.shape[0] * p.dtype.itemsize for p in params)


def _vmem_limit(block_bytes):
    return min(int(block_bytes * 1.25) + (8 << 20), V7X_VMEM_BYTES - (6 << 20))


def _group_in_columns(w_in):
    cw, rw = CONV_WIDTH, RET_WIDTH
    cols = []
    for j in range(CONV_BLOCKS):
        cols += [w_in[..., j * V7X_LANES:(j + 1) * V7X_LANES],
                 w_in[..., cw + j * V7X_LANES:cw + (j + 1) * V7X_LANES]]
    for h in range(RET_HEADS):
        cols += [w_in[..., 2 * cw + i * rw + h * RET_DIM:2 * cw + i * rw + (h + 1) * RET_DIM]
                 for i in range(4)]
    return jnp.concatenate(cols, axis=-1)


def _mixer_call(x, layer, params, cos2, sin2, mask, qdec, kdec, sdec):
    bsz, seq, d = x.shape
    ts = SEQ_TILE
    assert seq % ts == 0 and ts % RET_TILE == 0 and ts % CONV_ROWS == 0
    tile = pl.BlockSpec((None, ts, d), lambda b, t: (b, t, 0))
    tab = pl.BlockSpec((ts, RET_DIM), lambda b, t: (t, 0))
    tables = [mask, qdec, kdec, sdec]
    scratch = [
        pltpu.VMEM((ts, d), jnp.bfloat16),
        pltpu.VMEM((CONV_BLOCKS, HIST + ts, V7X_LANES), jnp.float32),
        pltpu.VMEM((ts, CONV_WIDTH), jnp.float32),
        pltpu.VMEM((RET_DIM, RET_WIDTH), jnp.float32),
        pltpu.VMEM((ts, d), jnp.bfloat16),
    ]
    nbytes = (4 * ts * d * 4 + _layer_bytes(params) + sum(a.size * a.dtype.itemsize for a in tables)
              + 4 * ts * RET_DIM * 4 + 2 * ts * d * 2 + ts * CONV_WIDTH * 4
              + (HIST + ts) * CONV_WIDTH * 4 + RET_DIM * RET_WIDTH * 4
              + 3 * ts * NCOL * 4)
    return pl.pallas_call(
        _mixer_kernel,
        grid=(bsz, seq // ts),
        in_specs=([tile] + [_layer_spec(p, layer) for p in params] + [tab, tab]
                  + [_const_spec(a.shape) for a in tables]),
        out_specs=tile,
        out_shape=jax.ShapeDtypeStruct(x.shape, x.dtype),
        scratch_shapes=scratch,
        compiler_params=pltpu.CompilerParams(
            dimension_semantics=("arbitrary", "arbitrary"),
            vmem_limit_bytes=_vmem_limit(nbytes)),
        name="mixer",
    )(x, *params, cos2, sin2, *tables)


def _ffn_call(x2d, layer, params, gf, final_norm):
    n, d = x2d.shape
    tm = FFN_TILE
    assert n % tm == 0
    tile = pl.BlockSpec((tm, d), lambda i: (i, 0))
    nbytes = (4 * tm * d * 4 + _layer_bytes(params) + gf.size * gf.dtype.itemsize
              + tm * d * 2 + tm * D_FF * 2)
    return pl.pallas_call(
        functools.partial(_ffn_kernel, final_norm=final_norm),
        grid=(n // tm,),
        in_specs=[tile] + [_layer_spec(p, layer) for p in params] + [_const_spec(gf.shape)],
        out_specs=tile,
        out_shape=jax.ShapeDtypeStruct(x2d.shape, x2d.dtype),
        scratch_shapes=[pltpu.VMEM((tm, d), jnp.bfloat16),
                        pltpu.VMEM((tm, D_FF), jnp.bfloat16)],
        compiler_params=pltpu.CompilerParams(
            dimension_semantics=("arbitrary",),
            vmem_limit_bytes=_vmem_limit(nbytes)),
        name="ffn",
    )(x2d, *params, gf)


def _tables(seq):
    f32 = jnp.float32
    half = RET_DIM // 2
    pos = jnp.arange(seq, dtype=f32)
    freqs = ROPE_BASE ** (-jnp.arange(half, dtype=f32) / half)
    ang = pos[:, None] * freqs[None, :]
    cos, sin = jnp.cos(ang), jnp.sin(ang)
    cos2 = jnp.concatenate([cos, cos], axis=-1)
    sin2 = jnp.concatenate([-sin, sin], axis=-1)

    log_g = jnp.log(1.0 - 2.0 ** (-5.0 - jnp.arange(RET_HEADS, dtype=f32)))
    idx = jnp.arange(RET_TILE, dtype=f32)
    diff = idx[:, None] - idx[None, :]
    chunk = jnp.arange(RET_TILE) // CHUNK
    same = chunk[:, None] == chunk[None, :]
    earlier = chunk[:, None] > chunk[None, :]
    lg = log_g[:, None, None]
    mask = jnp.where(same[None], jnp.exp(lg * jnp.abs(diff)[None]),
                     jnp.where(earlier[None], jnp.exp(lg * diff[None]), 0.0))
    key_scale = RET_DIM ** -0.5
    mask = mask * key_scale

    def lanes(per_head):
        return jnp.repeat(per_head.T, RET_DIM, axis=1)

    qdec = lanes(jnp.exp(log_g[:, None] * (idx[None, :] + 1.0)))
    kdec = lanes(jnp.exp(log_g[:, None] * (RET_TILE - 1.0 - idx[None, :]))) * key_scale
    sdec = jnp.repeat(jnp.exp(log_g * RET_TILE), RET_DIM)[None, :]
    return cos2, sin2, mask, qdec, kdec, sdec


def kernel(x, norm1_g, w_in, conv_w, conv_b, conv_ln_g, conv_ln_b, ret_gn_g, w_out,
           norm2_g, w_gate, w_up, w_down, final_g):
    bsz, seq, d = x.shape
    depth = w_in.shape[0]
    bf16 = jnp.bfloat16
    tables = _tables(seq)
    rows = lambda a: a.reshape(depth, 1, -1)
    mixer_params = [rows(norm1_g), _group_in_columns(w_in).astype(bf16), conv_w, rows(conv_b),
                    rows(conv_ln_g), rows(conv_ln_b), rows(ret_gn_g), w_out.astype(bf16)]
    ffn_params = [rows(norm2_g), w_gate.astype(bf16), w_up.astype(bf16), w_down.astype(bf16)]
    for l in range(depth):
        x = _mixer_call(x, l, mixer_params, *tables)
        x = _ffn_call(x.reshape(bsz * seq, d), l, ffn_params, final_g.reshape(1, -1),
                      final_norm=(l == depth - 1)).reshape(bsz, seq, d)
    return x
```

```python
import functools

import jax
import jax.numpy as jnp
from jax import lax
from jax.experimental import pallas as pl
from jax.experimental.pallas import tpu as pltpu

D_MODEL = 1024
CHUNK = 64
CONV_WIDTH = 512
CONV_K = 31
RET_HEADS = 4
RET_DIM = 128
RET_WIDTH = RET_HEADS * RET_DIM
IN_WIDTH = 2 * CONV_WIDTH + 4 * RET_WIDTH
D_FF = 2816
ROPE_BASE = 10000.0
EPS = 1e-6

V7X_LANES = 128
V7X_MXU_COLS = 256
V7X_VMEM_BYTES = 64 * 1024 * 1024

SEQ_TILE = 512
RET_TILE = 256
FFN_TILE = 512
HIST = 32
CONV_ROWS = 64
NCOL = V7X_MXU_COLS
CONV_BLOCKS = CONV_WIDTH // V7X_LANES


def _dot(a, b):
    return jnp.dot(a, b, preferred_element_type=jnp.float32)


def _sigmoid(x):
    return 1.0 / (1.0 + jnp.exp(-x))


def _silu(x):
    return x * _sigmoid(x)


def _rms_rows(x, g):
    ms = jnp.mean(x * x, axis=-1, keepdims=True)
    return x * lax.rsqrt(ms + EPS) * g


def _mixer_kernel(x_ref, g1_ref, w_in_ref, conv_w_ref, conv_b_ref, ln_g_ref, ln_b_ref,
                  gn_g_ref, w_out_ref, cos_ref, sin_ref, mask_ref, qdec_ref, kdec_ref,
                  sdec_ref, o_ref, h_scr, u_scr, y_scr, state_scr, mixed_scr):
    t = pl.program_id(1)
    ts = x_ref.shape[0]
    cw = CONV_WIDTH

    @pl.when(t == 0)
    def _():
        u_scr[:, 0:HIST, :] = jnp.zeros((CONV_BLOCKS, HIST, V7X_LANES), jnp.float32)
        state_scr[...] = jnp.zeros_like(state_scr)

    @pl.when(t != 0)
    def _():
        u_scr[:, 0:HIST, :] = u_scr[:, ts:ts + HIST, :]

    h_scr[...] = _rms_rows(x_ref[...], g1_ref[...]).astype(jnp.bfloat16)

    def project(c):
        return _dot(h_scr[...], w_in_ref[:, c * NCOL:(c + 1) * NCOL])

    first = HIST - (CONV_K - 1)
    for lb in range(CONV_BLOCKS):
        ab = project(lb)
        u_scr[lb, HIST:HIST + ts, :] = ab[:, 0:V7X_LANES] * _sigmoid(ab[:, V7X_LANES:NCOL])
        ls = slice(lb * V7X_LANES, (lb + 1) * V7X_LANES)
        for r0 in range(0, ts, CONV_ROWS):
            acc = jnp.broadcast_to(conv_b_ref[:, ls], (CONV_ROWS, V7X_LANES))
            for k in range(CONV_K):
                acc = acc + conv_w_ref[k:k + 1, ls] * u_scr[lb, r0 + first + k:r0 + first + k + CONV_ROWS, :]
            y_scr[r0:r0 + CONV_ROWS, ls] = acc

    for hd in range(RET_HEADS):
        hs = slice(hd * RET_DIM, (hd + 1) * RET_DIM)
        qk = project(CONV_BLOCKS + 2 * hd)
        vg = project(CONV_BLOCKS + 2 * hd + 1)
        for blk in range(ts // RET_TILE):
            rs = slice(blk * RET_TILE, (blk + 1) * RET_TILE)
            pos = pl.ds(pl.multiple_of(t * ts + blk * RET_TILE, RET_TILE), RET_TILE)
            cos = cos_ref[pos, :]
            sin = sin_ref[pos, :]
            q = qk[rs, 0:RET_DIM]
            k = qk[rs, RET_DIM:NCOL]
            v = vg[rs, 0:RET_DIM].astype(jnp.bfloat16)
            g = vg[rs, RET_DIM:NCOL]
            q = q * cos + pltpu.roll(q, RET_DIM // 2, axis=1) * sin
            k = k * cos + pltpu.roll(k, RET_DIM // 2, axis=1) * sin
            scores = lax.dot_general(q.astype(jnp.bfloat16), k.astype(jnp.bfloat16),
                                     (((1,), (1,)), ((), ())),
                                     preferred_element_type=jnp.float32)
            p = (scores * mask_ref[hd]).astype(jnp.bfloat16)
            state = state_scr[:, hs]
            r = _dot(p, v) + _dot((q * qdec_ref[:, hs]).astype(jnp.bfloat16),
                                  state.astype(jnp.bfloat16))
            kv = lax.dot_general((k * kdec_ref[:, hs]).astype(jnp.bfloat16), v,
                                 (((0,), (0,)), ((), ())),
                                 preferred_element_type=jnp.float32)
            state_scr[:, hs] = sdec_ref[:, hs] * state + kv
            mu = jnp.mean(r, axis=-1, keepdims=True)
            rc = r - mu
            var = jnp.mean(rc * rc, axis=-1, keepdims=True)
            rn = rc * lax.rsqrt(var + EPS)
            mixed_scr[rs, cw + hd * RET_DIM:cw + (hd + 1) * RET_DIM] = (
                rn * gn_g_ref[:, hs] * _silu(g)).astype(jnp.bfloat16)

    for r0 in range(0, ts, CONV_ROWS):
        y = y_scr[r0:r0 + CONV_ROWS, :]
        mu = jnp.mean(y, axis=-1, keepdims=True)
        yc = y - mu
        var = jnp.mean(yc * yc, axis=-1, keepdims=True)
        yn = yc * lax.rsqrt(var + EPS) * ln_g_ref[...] + ln_b_ref[...]
        mixed_scr[r0:r0 + CONV_ROWS, 0:cw] = _silu(yn).astype(jnp.bfloat16)

    for c in range(D_MODEL // NCOL):
        cs = slice(c * NCOL, (c + 1) * NCOL)
        o_ref[:, cs] = x_ref[:, cs] + _dot(mixed_scr[...], w_out_ref[:, cs])


def _ffn_kernel(x_ref, g2_ref, w_gate_ref, w_up_ref, w_down_ref, gf_ref, o_ref,
                h_scr, a_scr, *, final_norm):
    h_scr[...] = _rms_rows(x_ref[...], g2_ref[...]).astype(jnp.bfloat16)
    for c in range(D_FF // NCOL):
        cs = slice(c * NCOL, (c + 1) * NCOL)
        gate = _dot(h_scr[...], w_gate_ref[:, cs])
        up = _dot(h_scr[...], w_up_ref[:, cs])
        a_scr[:, cs] = (_silu(gate) * up).astype(jnp.bfloat16)
    for c in range(D_MODEL // NCOL):
        cs = slice(c * NCOL, (c + 1) * NCOL)
        o_ref[:, cs] = x_ref[:, cs] + _dot(a_scr[...], w_down_ref[:, cs])
    if final_norm:
        o_ref[...] = _rms_rows(o_ref[...], gf_ref[...])


def _const_spec(shape):
    nd = len(shape)
    return pl.BlockSpec(shape, lambda *_: (0,) * nd, pipeline_mode=pl.Buffered(1))


def _layer_spec(stacked, layer):
    nd = stacked.ndim - 1
    return pl.BlockSpec((None,) + stacked.shape[1:], lambda *_: (layer,) + (0,) * nd,
                        pipeline_mode=pl.Buffered(1))


def _layer_bytes(params):
    return sum(p.size // p.shape[0] * p.dtype.itemsize for p in params)


def _vmem_limit(block_bytes):
    return min(int(block_bytes * 1.25) + (8 << 20), V7X_VMEM_BYTES - (6 << 20))


def _group_in_columns(w_in):
    assert CONV_BLOCKS == RET_HEADS and V7X_LANES == RET_DIM
    lead = w_in.shape[:-1]
    streams = w_in.reshape(lead + (6, CONV_BLOCKS, V7X_LANES))
    conv = jnp.swapaxes(streams[..., 0:2, :, :], -3, -2).reshape(lead + (2 * CONV_WIDTH,))
    ret = jnp.swapaxes(streams[..., 2:6, :, :], -3, -2).reshape(lead + (4 * RET_WIDTH,))
    return jnp.concatenate([conv, ret], axis=-1)


def _mixer_call(x, layer, params, cos2, sin2, mask, qdec, kdec, sdec):
    bsz, seq, d = x.shape
    ts = SEQ_TILE
    assert seq % ts == 0 and ts % RET_TILE == 0 and ts % CONV_ROWS == 0
    tile = pl.BlockSpec((None, ts, d), lambda b, t: (b, t, 0))
    tables = [cos2, sin2, mask, qdec, kdec, sdec]
    scratch = [
        pltpu.VMEM((ts, d), jnp.bfloat16),
        pltpu.VMEM((CONV_BLOCKS, HIST + ts, V7X_LANES), jnp.float32),
        pltpu.VMEM((ts, CONV_WIDTH), jnp.float32),
        pltpu.VMEM((RET_DIM, RET_WIDTH), jnp.float32),
        pltpu.VMEM((ts, d), jnp.bfloat16),
    ]
    nbytes = (4 * ts * d * 4 + _layer_bytes(params) + sum(a.size * a.dtype.itemsize for a in tables)
              + 2 * ts * d * 2 + ts * CONV_WIDTH * 4
              + (HIST + ts) * CONV_WIDTH * 4 + RET_DIM * RET_WIDTH * 4
              + 3 * ts * NCOL * 4)
    return pl.pallas_call(
        _mixer_kernel,
        grid=(bsz, seq // ts),
        in_specs=([tile] + [_layer_spec(p, layer) for p in params]
                  + [_const_spec(a.shape) for a in tables]),
        out_specs=tile,
        out_shape=jax.ShapeDtypeStruct(x.shape, x.dtype),
        scratch_shapes=scratch,
        compiler_params=pltpu.CompilerParams(
            dimension_semantics=("arbitrary", "arbitrary"),
            vmem_limit_bytes=_vmem_limit(nbytes)),
        name="mixer",
    )(x, *params, *tables)


def _ffn_call(x2d, layer, params, gf, final_norm):
    n, d = x2d.shape
    tm = FFN_TILE
    assert n % tm == 0
    tile = pl.BlockSpec((tm, d), lambda i: (i, 0))
    nbytes = (4 * tm * d * 4 + _layer_bytes(params) + gf.size * gf.dtype.itemsize
              + tm * d * 2 + tm * D_FF * 2)
    return pl.pallas_call(
        functools.partial(_ffn_kernel, final_norm=final_norm),
        grid=(n // tm,),
        in_specs=[tile] + [_layer_spec(p, layer) for p in params] + [_const_spec(gf.shape)],
        out_specs=tile,
        out_shape=jax.ShapeDtypeStruct(x2d.shape, x2d.dtype),
        scratch_shapes=[pltpu.VMEM((tm, d), jnp.bfloat16),
                        pltpu.VMEM((tm, D_FF), jnp.bfloat16)],
        compiler_params=pltpu.CompilerParams(
            dimension_semantics=("arbitrary",),
            vmem_limit_bytes=_vmem_limit(nbytes)),
        name="ffn",
    )(x2d, *params, gf)


def _tables(seq):
    f32 = jnp.float32
    half = RET_DIM // 2
    pos = jnp.arange(seq, dtype=f32)
    freqs = ROPE_BASE ** (-jnp.arange(half, dtype=f32) / half)
    ang = pos[:, None] * freqs[None, :]
    cos, sin = jnp.cos(ang), jnp.sin(ang)
    cos2 = jnp.concatenate([cos, cos], axis=-1)
    sin2 = jnp.concatenate([-sin, sin], axis=-1)

    log_g = jnp.log(1.0 - 2.0 ** (-5.0 - jnp.arange(RET_HEADS, dtype=f32)))
    idx = jnp.arange(RET_TILE, dtype=f32)
    diff = idx[:, None] - idx[None, :]
    chunk = jnp.arange(RET_TILE) // CHUNK
    same = chunk[:, None] == chunk[None, :]
    earlier = chunk[:, None] > chunk[None, :]
    lg = log_g[:, None, None]
    mask = jnp.where(same[None], jnp.exp(lg * jnp.abs(diff)[None]),
                     jnp.where(earlier[None], jnp.exp(lg * diff[None]), 0.0))
    key_scale = RET_DIM ** -0.5
    mask = mask * key_scale

    def lanes(per_head):
        return jnp.repeat(per_head.T, RET_DIM, axis=1)

    qdec = lanes(jnp.exp(log_g[:, None] * (idx[None, :] + 1.0)))
    kdec = lanes(jnp.exp(log_g[:, None] * (RET_TILE - 1.0 - idx[None, :]))) * key_scale
    sdec = jnp.repeat(jnp.exp(log_g * RET_TILE), RET_DIM)[None, :]
    return cos2, sin2, mask, qdec, kdec, sdec


def kernel(x, norm1_g, w_in, conv_w, conv_b, conv_ln_g, conv_ln_b, ret_gn_g, w_out,
           norm2_g, w_gate, w_up, w_down, final_g):
    bsz, seq, d = x.shape
    depth = w_in.shape[0]
    bf16 = jnp.bfloat16
    tables = _tables(seq)
    rows = lambda a: a.reshape(depth, 1, -1)
    mixer_params = [rows(norm1_g), _group_in_columns(w_in).astype(bf16), conv_w, rows(conv_b),
                    rows(conv_ln_g), rows(conv_ln_b), rows(ret_gn_g), w_out.astype(bf16)]
    ffn_params = [rows(norm2_g), w_gate.astype(bf16), w_up.astype(bf16), w_down.astype(bf16)]
    for l in range(depth):
        x = _mixer_call(x, l, mixer_params, *tables)
        x = _ffn_call(x.reshape(bsz * seq, d), l, ffn_params, final_g.reshape(1, -1),
                      final_norm=(l == depth - 1)).reshape(bsz, seq, d)
    return x
```

```python
import functools

import jax
import jax.numpy as jnp
from jax import lax
from jax.experimental import pallas as pl
from jax.experimental.pallas import tpu as pltpu

D_MODEL = 1024
CHUNK = 64
CONV_WIDTH = 512
CONV_K = 31
RET_HEADS = 4
RET_DIM = 128
RET_WIDTH = RET_HEADS * RET_DIM
IN_WIDTH = 2 * CONV_WIDTH + 4 * RET_WIDTH
D_FF = 2816
ROPE_BASE = 10000.0
EPS = 1e-6

V7X_LANES = 128
V7X_MXU_COLS = 256
V7X_VMEM_BYTES = 64 * 1024 * 1024

SEQ_TILE = 512
RET_TILE = 256
FFN_TILE = 1024
HIST = 32
CONV_ROWS = 64
NCOL = V7X_MXU_COLS
CONV_BLOCKS = CONV_WIDTH // V7X_LANES


def _dot(a, b):
    return jnp.dot(a, b, preferred_element_type=jnp.float32)


def _sigmoid(x):
    return 1.0 / (1.0 + jnp.exp(-x))


def _silu(x):
    return x * _sigmoid(x)


def _rms_rows(x, g):
    ms = jnp.mean(x * x, axis=-1, keepdims=True)
    return x * lax.rsqrt(ms + EPS) * g


def _mixer_kernel(x_ref, g1_ref, w_in_ref, conv_w_ref, conv_b_ref, ln_g_ref, ln_b_ref,
                  gn_g_ref, w_out_ref, cos_ref, sin_ref, mask_ref, qdec_ref, kdec_ref,
                  sdec_ref, o_ref, h_scr, u_scr, y_scr, state_scr, mixed_scr):
    t = pl.program_id(1)
    ts = x_ref.shape[0]
    cw, rw = CONV_WIDTH, RET_WIDTH

    @pl.when(t == 0)
    def _():
        u_scr[:, 0:HIST, :] = jnp.zeros((CONV_BLOCKS, HIST, V7X_LANES), jnp.float32)
        state_scr[...] = jnp.zeros_like(state_scr)

    @pl.when(t != 0)
    def _():
        u_scr[:, 0:HIST, :] = u_scr[:, ts:ts + HIST, :]

    h_scr[...] = _rms_rows(x_ref[...], g1_ref[...]).astype(jnp.bfloat16)

    def project(col_a, col_b):
        w = jnp.concatenate([w_in_ref[:, col_a:col_a + V7X_LANES],
                             w_in_ref[:, col_b:col_b + V7X_LANES]], axis=1)
        return _dot(h_scr[...], w)

    first = HIST - (CONV_K - 1)
    for lb in range(CONV_BLOCKS):
        ab = project(lb * V7X_LANES, cw + lb * V7X_LANES)
        u_scr[lb, HIST:HIST + ts, :] = ab[:, 0:V7X_LANES] * _sigmoid(ab[:, V7X_LANES:NCOL])
        ls = slice(lb * V7X_LANES, (lb + 1) * V7X_LANES)
        for r0 in range(0, ts, CONV_ROWS):
            acc = jnp.broadcast_to(conv_b_ref[:, ls], (CONV_ROWS, V7X_LANES))
            for k in range(CONV_K):
                acc = acc + conv_w_ref[k:k + 1, ls] * u_scr[lb, r0 + first + k:r0 + first + k + CONV_ROWS, :]
            y_scr[r0:r0 + CONV_ROWS, ls] = acc

    qo, ko, vo, go = 2 * cw, 2 * cw + rw, 2 * cw + 2 * rw, 2 * cw + 3 * rw
    qk = [project(qo + hd * RET_DIM, ko + hd * RET_DIM) for hd in range(RET_HEADS)]
    vg = [project(vo + hd * RET_DIM, go + hd * RET_DIM) for hd in range(RET_HEADS)]
    for blk in range(ts // RET_TILE):
        rs = slice(blk * RET_TILE, (blk + 1) * RET_TILE)
        cos = cos_ref[rs, :]
        sin = sin_ref[rs, :]
        for hd in range(RET_HEADS):
            hs = slice(hd * RET_DIM, (hd + 1) * RET_DIM)
            q = qk[hd][rs, 0:RET_DIM]
            k = qk[hd][rs, RET_DIM:NCOL]
            v = vg[hd][rs, 0:RET_DIM].astype(jnp.bfloat16)
            g = vg[hd][rs, RET_DIM:NCOL]
            q = q * cos + pltpu.roll(q, RET_DIM // 2, axis=1) * sin
            k = k * cos + pltpu.roll(k, RET_DIM // 2, axis=1) * sin
            scores = lax.dot_general(q.astype(jnp.bfloat16), k.astype(jnp.bfloat16),
                                     (((1,), (1,)), ((), ())),
                                     preferred_element_type=jnp.float32)
            p = (scores * mask_ref[hd]).astype(jnp.bfloat16)
            state = state_scr[:, hs]
            r = _dot(p, v) + _dot((q * qdec_ref[:, hs]).astype(jnp.bfloat16),
                                  state.astype(jnp.bfloat16))
            kv = lax.dot_general((k * kdec_ref[:, hs]).astype(jnp.bfloat16), v,
                                 (((0,), (0,)), ((), ())),
                                 preferred_element_type=jnp.float32)
            state_scr[:, hs] = sdec_ref[:, hs] * state + kv
            mu = jnp.mean(r, axis=-1, keepdims=True)
            rc = r - mu
            var = jnp.mean(rc * rc, axis=-1, keepdims=True)
            rn = rc * lax.rsqrt(var + EPS)
            mixed_scr[rs, cw + hd * RET_DIM:cw + (hd + 1) * RET_DIM] = (
                rn * gn_g_ref[:, hs] * _silu(g)).astype(jnp.bfloat16)

        for r0 in range(blk * RET_TILE, (blk + 1) * RET_TILE, CONV_ROWS):
            y = y_scr[r0:r0 + CONV_ROWS, :]
            mu = jnp.mean(y, axis=-1, keepdims=True)
            yc = y - mu
            var = jnp.mean(yc * yc, axis=-1, keepdims=True)
            yn = yc * lax.rsqrt(var + EPS) * ln_g_ref[...] + ln_b_ref[...]
            mixed_scr[r0:r0 + CONV_ROWS, 0:cw] = _silu(yn).astype(jnp.bfloat16)

        for c in range(D_MODEL // NCOL):
            cs = slice(c * NCOL, (c + 1) * NCOL)
            o_ref[rs, cs] = x_ref[rs, cs] + _dot(mixed_scr[rs, :], w_out_ref[:, cs])


def _ffn_kernel(x_ref, g2_ref, w_gate_ref, w_up_ref, w_down_ref, gf_ref, o_ref,
                h_scr, a_scr, *, final_norm):
    h_scr[...] = _rms_rows(x_ref[...], g2_ref[...]).astype(jnp.bfloat16)
    for c in range(D_FF // NCOL):
        cs = slice(c * NCOL, (c + 1) * NCOL)
        gate = _dot(h_scr[...], w_gate_ref[:, cs])
        up = _dot(h_scr[...], w_up_ref[:, cs])
        a_scr[:, cs] = (_silu(gate) * up).astype(jnp.bfloat16)
    for c in range(D_MODEL // NCOL):
        cs = slice(c * NCOL, (c + 1) * NCOL)
        o_ref[:, cs] = x_ref[:, cs] + _dot(a_scr[...], w_down_ref[:, cs])
    if final_norm:
        o_ref[...] = _rms_rows(o_ref[...], gf_ref[...])


def _const_spec(shape):
    nd = len(shape)
    return pl.BlockSpec(shape, lambda *_: (0,) * nd, pipeline_mode=pl.Buffered(1))


def _layer_spec(stacked, layer):
    nd = stacked.ndim - 1
    return pl.BlockSpec((None,) + stacked.shape[1:], lambda *_: (layer,) + (0,) * nd,
                        pipeline_mode=pl.Buffered(1))


def _layer_bytes(params):
    return sum(p.size // p.shape[0] * p.dtype.itemsize for p in params)


def _vmem_limit(block_bytes):
    return min(int(block_bytes * 1.25) + (8 << 20), V7X_VMEM_BYTES - (6 << 20))


def _mixer_call(x, layer, params, cos2, sin2, mask, qdec, kdec, sdec):
    bsz, seq, d = x.shape
    ts = SEQ_TILE
    assert seq % ts == 0 and ts % RET_TILE == 0 and ts % CONV_ROWS == 0
    tile = pl.BlockSpec((None, ts, d), lambda b, t: (b, t, 0))
    tab = pl.BlockSpec((ts, RET_DIM), lambda b, t: (t, 0))
    tables = [mask, qdec, kdec, sdec]
    scratch = [
        pltpu.VMEM((ts, d), jnp.bfloat16),
        pltpu.VMEM((CONV_BLOCKS, HIST + ts, V7X_LANES), jnp.float32),
        pltpu.VMEM((ts, CONV_WIDTH), jnp.float32),
        pltpu.VMEM((RET_DIM, RET_WIDTH), jnp.float32),
        pltpu.VMEM((ts, d), jnp.bfloat16),
    ]
    nbytes = (4 * ts * d * 4 + _layer_bytes(params) + sum(a.size * a.dtype.itemsize for a in tables)
              + 4 * ts * RET_DIM * 4 + 2 * ts * d * 2 + ts * CONV_WIDTH * 4
              + (HIST + ts) * CONV_WIDTH * 4 + RET_DIM * RET_WIDTH * 4
              + 3 * ts * NCOL * 4)
    return pl.pallas_call(
        _mixer_kernel,
        grid=(bsz, seq // ts),
        in_specs=([tile] + [_layer_spec(p, layer) for p in params] + [tab, tab]
                  + [_const_spec(a.shape) for a in tables]),
        out_specs=tile,
        out_shape=jax.ShapeDtypeStruct(x.shape, x.dtype),
        scratch_shapes=scratch,
        compiler_params=pltpu.CompilerParams(
            dimension_semantics=("arbitrary", "arbitrary"),
            vmem_limit_bytes=_vmem_limit(nbytes)),
        name="mixer",
    )(x, *params, cos2, sin2, *tables)


def _ffn_call(x2d, layer, params, gf, final_norm):
    n, d = x2d.shape
    tm = FFN_TILE
    assert n % tm == 0
    tile = pl.BlockSpec((tm, d), lambda i: (i, 0))
    nbytes = (4 * tm * d * 4 + _layer_bytes(params) + gf.size * gf.dtype.itemsize
              + tm * d * 2 + tm * D_FF * 2)
    return pl.pallas_call(
        functools.partial(_ffn_kernel, final_norm=final_norm),
        grid=(n // tm,),
        in_specs=[tile] + [_layer_spec(p, layer) for p in params] + [_const_spec(gf.shape)],
        out_specs=tile,
        out_shape=jax.ShapeDtypeStruct(x2d.shape, x2d.dtype),
        scratch_shapes=[pltpu.VMEM((tm, d), jnp.bfloat16),
                        pltpu.VMEM((tm, D_FF), jnp.bfloat16)],
        compiler_params=pltpu.CompilerParams(
            dimension_semantics=("arbitrary",),
            vmem_limit_bytes=_vmem_limit(nbytes)),
        name="ffn",
    )(x2d, *params, gf)


def _tables(seq):
    f32 = jnp.float32
    half = RET_DIM // 2
    pos = jnp.arange(seq, dtype=f32)
    freqs = ROPE_BASE ** (-jnp.arange(half, dtype=f32) / half)
    ang = pos[:, None] * freqs[None, :]
    cos, sin = jnp.cos(ang), jnp.sin(ang)
    cos2 = jnp.concatenate([cos, cos], axis=-1)
    sin2 = jnp.concatenate([-sin, sin], axis=-1)

    log_g = jnp.log(1.0 - 2.0 ** (-5.0 - jnp.arange(RET_HEADS, dtype=f32)))
    idx = jnp.arange(RET_TILE, dtype=f32)
    diff = idx[:, None] - idx[None, :]
    chunk = jnp.arange(RET_TILE) // CHUNK
    same = chunk[:, None] == chunk[None, :]
    earlier = chunk[:, None] > chunk[None, :]
    lg = log_g[:, None, None]
    mask = jnp.where(same[None], jnp.exp(lg * jnp.abs(diff)[None]),
                     jnp.where(earlier[None], jnp.exp(lg * diff[None]), 0.0))
    key_scale = RET_DIM ** -0.5
    mask = mask * key_scale

    def lanes(per_head):
        return jnp.repeat(per_head.T, RET_DIM, axis=1)

    qdec = lanes(jnp.exp(log_g[:, None] * (idx[None, :] + 1.0)))
    kdec = lanes(jnp.exp(log_g[:, None] * (RET_TILE - 1.0 - idx[None, :]))) * key_scale
    sdec = jnp.repeat(jnp.exp(log_g * RET_TILE), RET_DIM)[None, :]
    return cos2, sin2, mask, qdec, kdec, sdec


def kernel(x, norm1_g, w_in, conv_w, conv_b, conv_ln_g, conv_ln_b, ret_gn_g, w_out,
           norm2_g, w_gate, w_up, w_down, final_g):
    bsz, seq, d = x.shape
    depth = w_in.shape[0]
    bf16 = jnp.bfloat16
    tables = _tables(seq)
    rows = lambda a: a.reshape(depth, 1, -1)
    mixer_params = [rows(norm1_g), w_in.astype(bf16), conv_w, rows(conv_b),
                    rows(conv_ln_g), rows(conv_ln_b), rows(ret_gn_g), w_out.astype(bf16)]
    ffn_params = [rows(norm2_g), w_gate.astype(bf16), w_up.astype(bf16), w_down.astype(bf16)]
    for l in range(depth):
        x = _mixer_call(x, l, mixer_params, *tables)
        x = _ffn_call(x.reshape(bsz * seq, d), l, ffn_params, final_g.reshape(1, -1),
                      final_norm=(l == depth - 1)).reshape(bsz, seq, d)
    return x
```

```python
import functools

import jax
import jax.numpy as jnp
from jax import lax
from jax.experimental import pallas as pl
from jax.experimental.pallas import tpu as pltpu

D_MODEL = 1024
CHUNK = 64
CONV_WIDTH = 512
CONV_K = 31
RET_HEADS = 4
RET_DIM = 128
RET_WIDTH = RET_HEADS * RET_DIM
IN_WIDTH = 2 * CONV_WIDTH + 4 * RET_WIDTH
D_FF = 2816
ROPE_BASE = 10000.0
EPS = 1e-6

V7X_LANES = 128
V7X_MXU_COLS = 256
V7X_VMEM_BYTES = 64 * 1024 * 1024

SEQ_TILE = 1024
RET_TILE = 256
FFN_TILE = 1024
HIST = 32
CONV_ROWS = 64
NCOL = V7X_MXU_COLS
CONV_BLOCKS = CONV_WIDTH // V7X_LANES


def _dot(a, b):
    return jnp.dot(a, b, preferred_element_type=jnp.float32)


def _sigmoid(x):
    return 1.0 / (1.0 + jnp.exp(-x))


def _silu(x):
    return x * _sigmoid(x)


def _rms_rows(x, g):
    ms = jnp.mean(x * x, axis=-1, keepdims=True)
    return x * lax.rsqrt(ms + EPS) * g


def _mixer_kernel(x_ref, g1_ref, w_in_ref, conv_w_ref, conv_b_ref, ln_g_ref, ln_b_ref,
                  gn_g_ref, w_out_ref, cos_ref, sin_ref, mask_ref, qdec_ref, kdec_ref,
                  sdec_ref, o_ref, h_scr, u_scr, y_scr, state_scr, mixed_scr):
    t = pl.program_id(1)
    ts = x_ref.shape[0]
    cw, rw = CONV_WIDTH, RET_WIDTH

    @pl.when(t == 0)
    def _():
        u_scr[:, 0:HIST, :] = jnp.zeros((CONV_BLOCKS, HIST, V7X_LANES), jnp.float32)
        state_scr[...] = jnp.zeros_like(state_scr)

    @pl.when(t != 0)
    def _():
        u_scr[:, 0:HIST, :] = u_scr[:, ts:ts + HIST, :]

    h_scr[...] = _rms_rows(x_ref[...], g1_ref[...]).astype(jnp.bfloat16)

    def project(col_a, col_b):
        w = jnp.concatenate([w_in_ref[:, col_a:col_a + V7X_LANES],
                             w_in_ref[:, col_b:col_b + V7X_LANES]], axis=1)
        return _dot(h_scr[...], w)

    first = HIST - (CONV_K - 1)
    for lb in range(CONV_BLOCKS):
        ab = project(lb * V7X_LANES, cw + lb * V7X_LANES)
        u_scr[lb, HIST:HIST + ts, :] = ab[:, 0:V7X_LANES] * _sigmoid(ab[:, V7X_LANES:NCOL])
        ls = slice(lb * V7X_LANES, (lb + 1) * V7X_LANES)
        for r0 in range(0, ts, CONV_ROWS):
            acc = jnp.broadcast_to(conv_b_ref[:, ls], (CONV_ROWS, V7X_LANES))
            for k in range(CONV_K):
                acc = acc + conv_w_ref[k:k + 1, ls] * u_scr[lb, r0 + first + k:r0 + first + k + CONV_ROWS, :]
            y_scr[r0:r0 + CONV_ROWS, ls] = acc

    qo, ko, vo, go = 2 * cw, 2 * cw + rw, 2 * cw + 2 * rw, 2 * cw + 3 * rw
    qk = [project(qo + hd * RET_DIM, ko + hd * RET_DIM) for hd in range(RET_HEADS)]
    vg = [project(vo + hd * RET_DIM, go + hd * RET_DIM) for hd in range(RET_HEADS)]
    for blk in range(ts // RET_TILE):
        rs = slice(blk * RET_TILE, (blk + 1) * RET_TILE)
        cos = cos_ref[rs, :]
        sin = sin_ref[rs, :]
        for hd in range(RET_HEADS):
            hs = slice(hd * RET_DIM, (hd + 1) * RET_DIM)
            q = qk[hd][rs, 0:RET_DIM]
            k = qk[hd][rs, RET_DIM:NCOL]
            v = vg[hd][rs, 0:RET_DIM].astype(jnp.bfloat16)
            g = vg[hd][rs, RET_DIM:NCOL]
            q = q * cos + pltpu.roll(q, RET_DIM // 2, axis=1) * sin
            k = k * cos + pltpu.roll(k, RET_DIM // 2, axis=1) * sin
            scores = lax.dot_general(q.astype(jnp.bfloat16), k.astype(jnp.bfloat16),
                                     (((1,), (1,)), ((), ())),
                                     preferred_element_type=jnp.float32)
            p = (scores * mask_ref[hd]).astype(jnp.bfloat16)
            state = state_scr[:, hs]
            r = _dot(p, v) + _dot((q * qdec_ref[:, hs]).astype(jnp.bfloat16),
                                  state.astype(jnp.bfloat16))
            kv = lax.dot_general((k * kdec_ref[:, hs]).astype(jnp.bfloat16), v,
                                 (((0,), (0,)), ((), ())),
                                 preferred_element_type=jnp.float32)
            state_scr[:, hs] = sdec_ref[:, hs] * state + kv
            mu = jnp.mean(r, axis=-1, keepdims=True)
            rc = r - mu
            var = jnp.mean(rc * rc, axis=-1, keepdims=True)
            rn = rc * lax.rsqrt(var + EPS)
            mixed_scr[rs, cw + hd * RET_DIM:cw + (hd + 1) * RET_DIM] = (
                rn * gn_g_ref[:, hs] * _silu(g)).astype(jnp.bfloat16)

        for r0 in range(blk * RET_TILE, (blk + 1) * RET_TILE, CONV_ROWS):
            y = y_scr[r0:r0 + CONV_ROWS, :]
            mu = jnp.mean(y, axis=-1, keepdims=True)
            yc = y - mu
            var = jnp.mean(yc * yc, axis=-1, keepdims=True)
            yn = yc * lax.rsqrt(var + EPS) * ln_g_ref[...] + ln_b_ref[...]
            mixed_scr[r0:r0 + CONV_ROWS, 0:cw] = _silu(yn).astype(jnp.bfloat16)

        for c in range(D_MODEL // NCOL):
            cs = slice(c * NCOL, (c + 1) * NCOL)
            o_ref[rs, cs] = x_ref[rs, cs] + _dot(mixed_scr[rs, :], w_out_ref[:, cs])


def _ffn_kernel(x_ref, g2_ref, w_gate_ref, w_up_ref, w_down_ref, gf_ref, o_ref,
                h_scr, a_scr, *, final_norm):
    h_scr[...] = _rms_rows(x_ref[...], g2_ref[...]).astype(jnp.bfloat16)
    for c in range(D_FF // NCOL):
        cs = slice(c * NCOL, (c + 1) * NCOL)
        gate = _dot(h_scr[...], w_gate_ref[:, cs])
        up = _dot(h_scr[...], w_up_ref[:, cs])
        a_scr[:, cs] = (_silu(gate) * up).astype(jnp.bfloat16)
    for c in range(D_MODEL // NCOL):
        cs = slice(c * NCOL, (c + 1) * NCOL)
        o_ref[:, cs] = x_ref[:, cs] + _dot(a_scr[...], w_down_ref[:, cs])
    if final_norm:
        o_ref[...] = _rms_rows(o_ref[...], gf_ref[...])


def _const_spec(shape):
    nd = len(shape)
    return pl.BlockSpec(shape, lambda *_: (0,) * nd, pipeline_mode=pl.Buffered(1))


def _layer_spec(stacked, layer):
    nd = stacked.ndim - 1
    return pl.BlockSpec((None,) + stacked.shape[1:], lambda *_: (layer,) + (0,) * nd,
                        pipeline_mode=pl.Buffered(1))


def _layer_bytes(params):
    return sum(p.size // p.shape[0] * p.dtype.itemsize for p in params)


def _vmem_limit(block_bytes):
    return min(int(block_bytes * 1.25) + (8 << 20), V7X_VMEM_BYTES - (6 << 20))


def _mixer_call(x, layer, params, cos2, sin2, mask, qdec, kdec, sdec):
    bsz, seq, d = x.shape
    ts = SEQ_TILE
    assert seq % ts == 0 and ts % RET_TILE == 0 and ts % CONV_ROWS == 0
    tile = pl.BlockSpec((None, ts, d), lambda b, t: (b, t, 0))
    tab = pl.BlockSpec((ts, RET_DIM), lambda b, t: (t, 0))
    tables = [mask, qdec, kdec, sdec]
    scratch = [
        pltpu.VMEM((ts, d), jnp.bfloat16),
        pltpu.VMEM((CONV_BLOCKS, HIST + ts, V7X_LANES), jnp.float32),
        pltpu.VMEM((ts, CONV_WIDTH), jnp.float32),
        pltpu.VMEM((RET_DIM, RET_WIDTH), jnp.float32),
        pltpu.VMEM((ts, d), jnp.bfloat16),
    ]
    nbytes = (4 * ts * d * 4 + _layer_bytes(params) + sum(a.size * a.dtype.itemsize for a in tables)
              + 4 * ts * RET_DIM * 4 + 2 * ts * d * 2 + ts * CONV_WIDTH * 4
              + (HIST + ts) * CONV_WIDTH * 4 + RET_DIM * RET_WIDTH * 4
              + 3 * ts * NCOL * 4)
    return pl.pallas_call(
        _mixer_kernel,
        grid=(bsz, seq // ts),
        in_specs=([tile] + [_layer_spec(p, layer) for p in params] + [tab, tab]
                  + [_const_spec(a.shape) for a in tables]),
        out_specs=tile,
        out_shape=jax.ShapeDtypeStruct(x.shape, x.dtype),
        scratch_shapes=scratch,
        compiler_params=pltpu.CompilerParams(
            dimension_semantics=("arbitrary", "arbitrary"),
            vmem_limit_bytes=_vmem_limit(nbytes)),
        name="mixer",
    )(x, *params, cos2, sin2, *tables)


def _ffn_call(x2d, layer, params, gf, final_norm):
    n, d = x2d.shape
    tm = FFN_TILE
    assert n % tm == 0
    tile = pl.BlockSpec((tm, d), lambda i: (i, 0))
    nbytes = (4 * tm * d * 4 + _layer_bytes(params) + gf.size * gf.dtype.itemsize
              + tm * d * 2 + tm * D_FF * 2)
    return pl.pallas_call(
        functools.partial(_ffn_kernel, final_norm=final_norm),
        grid=(n // tm,),
        in_specs=[tile] + [_layer_spec(p, layer) for p in params] + [_const_spec(gf.shape)],
        out_specs=tile,
        out_shape=jax.ShapeDtypeStruct(x2d.shape, x2d.dtype),
        scratch_shapes=[pltpu.VMEM((tm, d), jnp.bfloat16),
                        pltpu.VMEM((tm, D_FF), jnp.bfloat16)],
        compiler_params=pltpu.CompilerParams(
            dimension_semantics=("arbitrary",),
            vmem_limit_bytes=_vmem_limit(nbytes)),
        name="ffn",
    )(x2d, *params, gf)


def _tables(seq):
    f32 = jnp.float32
    half = RET_DIM // 2
    pos = jnp.arange(seq, dtype=f32)
    freqs = ROPE_BASE ** (-jnp.arange(half, dtype=f32) / half)
    ang = pos[:, None] * freqs[None, :]
    cos, sin = jnp.cos(ang), jnp.sin(ang)
    cos2 = jnp.concatenate([cos, cos], axis=-1)
    sin2 = jnp.concatenate([-sin, sin], axis=-1)

    log_g = jnp.log(1.0 - 2.0 ** (-5.0 - jnp.arange(RET_HEADS, dtype=f32)))
    idx = jnp.arange(RET_TILE, dtype=f32)
    diff = idx[:, None] - idx[None, :]
    chunk = jnp.arange(RET_TILE) // CHUNK
    same = chunk[:, None] == chunk[None, :]
    earlier = chunk[:, None] > chunk[None, :]
    lg = log_g[:, None, None]
    mask = jnp.where(same[None], jnp.exp(lg * jnp.abs(diff)[None]),
                     jnp.where(earlier[None], jnp.exp(lg * diff[None]), 0.0))
    key_scale = RET_DIM ** -0.5
    mask = mask * key_scale

    def lanes(per_head):
        return jnp.repeat(per_head.T, RET_DIM, axis=1)

    qdec = lanes(jnp.exp(log_g[:, None] * (idx[None, :] + 1.0)))
    kdec = lanes(jnp.exp(log_g[:, None] * (RET_TILE - 1.0 - idx[None, :]))) * key_scale
    sdec = jnp.repeat(jnp.exp(log_g * RET_TILE), RET_DIM)[None, :]
    return cos2, sin2, mask, qdec, kdec, sdec


def kernel(x, norm1_g, w_in, conv_w, conv_b, conv_ln_g, conv_ln_b, ret_gn_g, w_out,
           norm2_g, w_gate, w_up, w_down, final_g):
    bsz, seq, d = x.shape
    depth = w_in.shape[0]
    bf16 = jnp.bfloat16
    tables = _tables(seq)
    rows = lambda a: a.reshape(depth, 1, -1)
    mixer_params = [rows(norm1_g), w_in.astype(bf16), conv_w, rows(conv_b),
                    rows(conv_ln_g), rows(conv_ln_b), rows(ret_gn_g), w_out.astype(bf16)]
    ffn_params = [rows(norm2_g), w_gate.astype(bf16), w_up.astype(bf16), w_down.astype(bf16)]
    for l in range(depth):
        x = _mixer_call(x, l, mixer_params, *tables)
        x = _ffn_call(x.reshape(bsz * seq, d), l, ffn_params, final_g.reshape(1, -1),
                      final_norm=(l == depth - 1)).reshape(bsz, seq, d)
    return x
```
